```python
import functools
import jax, jax.numpy as jnp
from jax import lax
import numpy as np

D_MODEL = 4096
BATCH = 4
SEQ = 2048
DEPTH = 2
DEC_BATCH = 8
DEC_SEQ = 1
PAST_LEN = 16384
PAGE_SIZE = 128

HEAD_DIM = 128
A_HEADS = (3 * D_MODEL) // (8 * HEAD_DIM)
A_WIDTH = A_HEADS * HEAD_DIM
C_HEADS = (3 * D_MODEL) // (8 * HEAD_DIM)
C_WIDTH = C_HEADS * HEAD_DIM
B_WIDTH = D_MODEL - A_WIDTH - C_WIDTH
POOL_WINDOWS = (2, 4, 8, 16)
POOL_GROUPS = len(POOL_WINDOWS)
POOL_GW = B_WIDTH // POOL_GROUPS
POOL_HIST = max(POOL_WINDOWS) - 1
IDX_HEADS = 32
IDX_DIM = 128
TOPK_MAX = 256
ROPE_THETA = 10000.0
EPS = 1e-6
Q_BLOCK = 128
SPARSE_BLOCK = 32
FORGET_BIAS_INIT = 2.0
COL_SIZES = (A_WIDTH, A_WIDTH, A_WIDTH, A_WIDTH, A_HEADS,
             B_WIDTH, B_WIDTH,
             C_WIDTH, C_WIDTH, C_WIDTH, C_WIDTH, IDX_HEADS * IDX_DIM, IDX_DIM, IDX_HEADS)
IN_COLS = sum(COL_SIZES)

kernel_name = "hymba_fox_pool_dsa_decode_step"


def rmsnorm(x, g):
    xf = x.astype(jnp.float32)
    y = xf * lax.rsqrt(jnp.mean(xf * xf, axis=-1, keepdims=True) + EPS)
    return (y * g.astype(jnp.float32)).astype(x.dtype)


def rope(x, pos):
    half = x.shape[-1] // 2
    inv = jnp.power(ROPE_THETA, -jnp.arange(half, dtype=jnp.float32) / half)
    ang = pos.astype(jnp.float32)[:, None] * inv[None, :]
    cos = jnp.cos(ang)[:, None, :]
    sin = jnp.sin(ang)[:, None, :]
    xf = x.astype(jnp.float32)
    x1, x2 = xf[..., :half], xf[..., half:]
    return jnp.concatenate([x1 * cos - x2 * sin, x2 * cos + x1 * sin], axis=-1).astype(x.dtype)


def project(h, w_in, b_f):
    B, L, _ = h.shape
    offs = np.cumsum(COL_SIZES)[:-1].tolist()
    (aq, ak, av, ag, af, bu, bg, cq, ck, cv, cg, iq, ik, iw) = jnp.split(
        jnp.einsum('bld,dn->bln', h, w_in), offs, axis=-1)
    heads = lambda t, n: t.reshape(B, L, n, -1)
    logf = jax.nn.log_sigmoid(af.astype(jnp.float32) + b_f.astype(jnp.float32))
    return (heads(aq, A_HEADS), heads(ak, A_HEADS), heads(av, A_HEADS), ag, logf, bu, bg,
            heads(cq, C_HEADS), heads(ck, C_HEADS), heads(cv, C_HEADS), cg,
            heads(iq, IDX_HEADS), ik, iw)


def fox_attend(q, k, v, cq, ckT, qpos, kpos):
    s = jnp.einsum('bqhd,bkhd->bhqk', q, k).astype(jnp.float32) * (HEAD_DIM ** -0.5)
    s = s + jnp.swapaxes(cq, 1, 2)[..., None] - ckT[:, :, None, :]
    s = jnp.where(kpos[None, :] <= qpos[:, None], s, -jnp.inf)
    p = jax.nn.softmax(s, axis=-1)
    return jnp.einsum('bhqk,bkhd->bqhd', p.astype(v.dtype), v)


def fox_prompt(q, k, v, logf):
    B, S, H, D = q.shape
    c = jnp.cumsum(logf, axis=1)
    cT = jnp.swapaxes(c, 1, 2)
    nb = S // Q_BLOCK
    kpos = jnp.arange(S)
    qb = jnp.swapaxes(q.reshape(B, nb, Q_BLOCK, H, D), 0, 1)
    cb = jnp.swapaxes(c.reshape(B, nb, Q_BLOCK, H), 0, 1)

    def block(args):
        qi, ci, i = args
        qpos = i * Q_BLOCK + jnp.arange(Q_BLOCK)
        return fox_attend(qi, k, v, ci, cT, qpos, kpos)

    out = lax.map(block, (qb, cb, jnp.arange(nb)))
    return jnp.swapaxes(out, 0, 1).reshape(B, S, H, D)


def pool_mix(u, hist, start, w_pool, pool_scale):
    B, L, _ = u.shape
    ext = jnp.concatenate([hist.astype(u.dtype), u], axis=1)
    cs = jnp.cumsum(ext.astype(jnp.float32), axis=1)
    cs = jnp.concatenate([jnp.zeros((B, 1, B_WIDTH), jnp.float32), cs], axis=1)
    end = cs[:, POOL_HIST + 1:POOL_HIST + 1 + L]
    pos = start + jnp.arange(L)
    uf = u.astype(jnp.float32)
    outs = []
    for g, w in enumerate(POOL_WINDOWS):
        sl = slice(g * POOL_GW, (g + 1) * POOL_GW)
        begin = cs[:, POOL_HIST + 1 - w:POOL_HIST + 1 - w + L, sl]
        cnt = jnp.minimum(w, pos + 1).astype(jnp.float32)[None, :, None]
        d = (end[..., sl] - begin) / cnt - uf[..., sl]
        outs.append(jnp.einsum('blc,cd->bld', d.astype(u.dtype), w_pool[g]))
    y = jnp.concatenate(outs, axis=-1) * pool_scale
    return y, ext[:, -POOL_HIST:]


def dsa_attend(q, iq, iw, ik, qpos, kpos, gather_kv, topk):
    rel = jax.nn.relu(jnp.einsum('bqhd,bkd->bqhk', iq, ik).astype(jnp.float32))
    score = jnp.einsum('bqh,bqhk->bqk', iw.astype(jnp.float32), rel)
    score = jnp.where(kpos[None, :] <= qpos[:, None], score, -jnp.inf)
    _, idx = lax.top_k(score, topk)
    valid = idx <= qpos[None, :, None]
    kv = gather_kv(idx)
    s = jnp.einsum('bqhd,bqshd->bqhs', q, kv[..., 0, :, :].astype(q.dtype)).astype(jnp.float32) * (HEAD_DIM ** -0.5)
    s = jnp.where(valid[:, :, None, :], s, -jnp.inf)
    p = jax.nn.softmax(s, axis=-1)
    return jnp.einsum('bqhs,bqshd->bqhd', p.astype(q.dtype), kv[..., 1, :, :].astype(q.dtype))


def dsa_prompt(q, kv, iq, iw, ik):
    B, S = q.shape[:2]
    topk = min(TOPK_MAX, S // 4)
    nb = S // SPARSE_BLOCK
    kpos = jnp.arange(S)
    gather = lambda idx: jax.vmap(lambda kvb, ib: kvb[ib])(kv, idx)
    to_blocks = lambda t: jnp.swapaxes(t.reshape((B, nb, SPARSE_BLOCK) + t.shape[2:]), 0, 1)

    def block(args):
        qi, iqi, iwi, i = args
        qpos = i * SPARSE_BLOCK + jnp.arange(SPARSE_BLOCK)
        return dsa_attend(qi, iqi, iwi, ik, qpos, kpos, gather, topk)

    out = lax.map(block, (to_blocks(q), to_blocks(iq), to_blocks(iw), jnp.arange(nb)))
    return jnp.swapaxes(out, 0, 1).reshape(B, S, C_HEADS, HEAD_DIM)


def gather_paged(pool, layer, page_table, new_kv, idx):
    Bd = idx.shape[0]
    past = jnp.minimum(idx, PAST_LEN - 1)
    phys = page_table[jnp.arange(Bd)[:, None, None], past // PAGE_SIZE]
    from_pool = pool[layer, phys, past % PAGE_SIZE]
    new_i = jnp.clip(idx - PAST_LEN, 0, new_kv.shape[1] - 1)
    from_new = jax.vmap(lambda nb, ib: nb[ib])(new_kv, new_i)
    is_new = (idx >= PAST_LEN)[..., None, None, None]
    return jnp.where(is_new, from_new.astype(from_pool.dtype), from_pool)


def merge(x, a_out, ag, b_out, bg, c_out, cg, w_out, g_post):
    B, L, _ = x.shape
    mix = jnp.concatenate([a_out.reshape(B, L, A_WIDTH) * jax.nn.silu(ag),
                           b_out.astype(x.dtype) * jax.nn.silu(bg),
                           c_out.reshape(B, L, C_WIDTH) * jax.nn.silu(cg)], axis=-1)
    return x + rmsnorm(jnp.einsum('bln,nd->bld', mix, w_out), g_post).astype(x.dtype)


def setup_inputs(seed: int = 0) -> dict:
    key = jax.random.key(seed)
    ks = jax.random.split(key, 16)
    n_pages = PAST_LEN // PAGE_SIZE
    n_used = DEC_BATCH * n_pages
    n_pool = n_used + max(1, n_used // 4)
    perm = jax.random.permutation(ks[0], n_pool)
    page_table = perm[:n_used].reshape(DEC_BATCH, n_pages).astype(jnp.int32)
    nrm = jax.random.normal
    return {
        "x_prompt": nrm(ks[1], (BATCH, SEQ, D_MODEL), jnp.float32),
        "x_sample": nrm(ks[2], (DEC_BATCH, DEC_SEQ, D_MODEL), jnp.float32),
        "cache_a_kv": nrm(ks[3], (DEPTH, n_pool, PAGE_SIZE, 2, A_HEADS, HEAD_DIM), jnp.float32),
        "cache_a_logf": jax.nn.log_sigmoid(FORGET_BIAS_INIT + nrm(ks[4], (DEPTH, n_pool, PAGE_SIZE, A_HEADS), jnp.float32)),
        "cache_c_kv": nrm(ks[5], (DEPTH, n_pool, PAGE_SIZE, 2, C_HEADS, HEAD_DIM), jnp.float32),
        "cache_c_idxk": nrm(ks[6], (DEPTH, n_pool, PAGE_SIZE, IDX_DIM), jnp.float32),
        "state_pool": nrm(ks[7], (DEPTH, DEC_BATCH, POOL_HIST, B_WIDTH), jnp.float32),
        "page_table": page_table,
        "g_pre": 1.0 + 0.1 * nrm(ks[8], (DEPTH, D_MODEL), jnp.float32),
        "w_in": nrm(ks[9], (DEPTH, D_MODEL, IN_COLS), jnp.float32) * (D_MODEL ** -0.5),
        "b_f": FORGET_BIAS_INIT + 0.1 * nrm(ks[10], (DEPTH, A_HEADS), jnp.float32),
        "w_pool": nrm(ks[11], (DEPTH, POOL_GROUPS, POOL_GW, POOL_GW), jnp.float32) * (POOL_GW ** -0.5),
        "pool_scale": 1.0 + 0.1 * nrm(ks[12], (DEPTH, B_WIDTH), jnp.float32),
        "w_out": nrm(ks[13], (DEPTH, D_MODEL, D_MODEL), jnp.float32) * (D_MODEL ** -0.5),
        "g_post": 1.0 + 0.1 * nrm(ks[14], (DEPTH, D_MODEL), jnp.float32),
    }


def reference(x_prompt, x_sample, cache_a_kv, cache_a_logf, cache_c_kv, cache_c_idxk, state_pool,
              page_table, g_pre, w_in, b_f, w_pool, pool_scale, w_out, g_post):
    B, S, _ = x_prompt.shape
    Bd, T, _ = x_sample.shape
    pos_p = jnp.arange(S)
    pos_s = PAST_LEN + jnp.arange(T)
    kpos_s = jnp.arange(PAST_LEN + T)
    topk_s = min(TOPK_MAX, (PAST_LEN + T) // 4)
    xp, xs = x_prompt, x_sample
    p_akv, p_alogf, p_ckv, p_cik, p_pool = [], [], [], [], []
    s_akv, s_alogf, s_ckv, s_cik, s_pool = [], [], [], [], []
    for l in range(DEPTH):
        h = rmsnorm(xp, g_pre[l])
        aq, ak, av, ag, logf, bu, bg, cq, ck, cv, cg, iq, ik, iw = project(h, w_in[l], b_f[l])
        a_out = fox_prompt(aq, ak, av, logf)
        b_out, pool_new = pool_mix(bu, jnp.zeros((B, POOL_HIST, B_WIDTH), bu.dtype), 0, w_pool[l], pool_scale[l])
        ckv = jnp.stack([rope(ck, pos_p), cv], axis=2)
        ik = rope(ik[:, :, None, :], pos_p)[:, :, 0, :]
        c_out = dsa_prompt(rope(cq, pos_p), ckv, rope(iq, pos_p), iw, ik)
        xp = merge(xp, a_out, ag, b_out, bg, c_out, cg, w_out[l], g_post[l])
        p_akv.append(jnp.stack([ak, av], axis=2))
        p_alogf.append(logf)
        p_ckv.append(ckv)
        p_cik.append(ik)
        p_pool.append(pool_new)

        h = rmsnorm(xs, g_pre[l])
        aq, ak, av, ag, logf, bu, bg, cq, ck, cv, cg, iq, ik, iw = project(h, w_in[l], b_f[l])
        past_akv = cache_a_kv[l, page_table].reshape(Bd, PAST_LEN, 2, A_HEADS, HEAD_DIM)
        past_logf = cache_a_logf[l, page_table].reshape(Bd, PAST_LEN, A_HEADS).astype(jnp.float32)
        c_all = jnp.cumsum(jnp.concatenate([past_logf, logf], axis=1), axis=1)
        k_all = jnp.concatenate([past_akv[:, :, 0].astype(ak.dtype), ak], axis=1)
        v_all = jnp.concatenate([past_akv[:, :, 1].astype(av.dtype), av], axis=1)
        a_out = fox_attend(aq, k_all, v_all, c_all[:, PAST_LEN:], jnp.swapaxes(c_all, 1, 2), pos_s, kpos_s)
        b_out, pool_new = pool_mix(bu, state_pool[l], PAST_LEN, w_pool[l], pool_scale[l])
        ckv = jnp.stack([rope(ck, pos_s), cv], axis=2)
        ik = rope(ik[:, :, None, :], pos_s)[:, :, 0, :]
        past_ik = cache_c_idxk[l, page_table].reshape(Bd, PAST_LEN, IDX_DIM)
        ik_all = jnp.concatenate([past_ik.astype(ik.dtype), ik], axis=1)
        gather = functools.partial(gather_paged, cache_c_kv, l, page_table, ckv)
        c_out = dsa_attend(rope(cq, pos_s), rope(iq, pos_s), iw, ik_all, pos_s, kpos_s, gather, topk_s)
        xs = merge(xs, a_out, ag, b_out, bg, c_out, cg, w_out[l], g_post[l])
        s_akv.append(jnp.stack([ak, av], axis=2))
        s_alogf.append(logf)
        s_ckv.append(ckv)
        s_cik.append(ik)
        s_pool.append(pool_new)

    return (xp, xs,
            jnp.stack(p_akv), jnp.stack(p_alogf), jnp.stack(p_ckv), jnp.stack(p_cik), jnp.stack(p_pool),
            jnp.stack(s_akv), jnp.stack(s_alogf), jnp.stack(s_ckv), jnp.stack(s_cik), jnp.stack(s_pool))
```

```python
import functools

import numpy as np
import jax
import jax.numpy as jnp
from jax import lax
from jax.experimental import pallas as pl
from jax.experimental.pallas import tpu as pltpu

F32 = jnp.float32
BF16 = jnp.bfloat16
I32 = jnp.int32

D_MODEL = 4096
HEAD_DIM = 128
A_HEADS = 12
A_WIDTH = A_HEADS * HEAD_DIM
C_HEADS = 12
C_WIDTH = C_HEADS * HEAD_DIM
B_WIDTH = D_MODEL - A_WIDTH - C_WIDTH
POOL_WINDOWS = (2, 4, 8, 16)
POOL_GROUPS = len(POOL_WINDOWS)
POOL_GW = B_WIDTH // POOL_GROUPS
POOL_HIST = max(POOL_WINDOWS) - 1
IDX_HEADS = 32
IDX_DIM = 128
TOPK_MAX = 256
ROPE_THETA = 10000.0
EPS = 1e-6
PAGE_SIZE = 128
SCALE = HEAD_DIM ** -0.5

LANE = 128
HEAD_ROWS = 16
SAMPLE_ROWS = 16
VMEM_LIMIT = 56 * 1024 * 1024
NEG_BIG = -1e30
INT_MIN = -2 ** 31

O_AQ = 0
O_AK = O_AQ + A_WIDTH
O_AV = O_AK + A_WIDTH
O_AG = O_AV + A_WIDTH
O_AF = O_AG + A_WIDTH
O_BU = O_AF + A_HEADS
O_BG = O_BU + B_WIDTH
O_CQ = O_BG + B_WIDTH
O_CK = O_CQ + C_WIDTH
O_CV = O_CK + C_WIDTH
O_CG = O_CV + C_WIDTH
O_IQ = O_CG + C_WIDTH
O_IK = O_IQ + IDX_HEADS * IDX_DIM
O_IW = O_IK + IDX_DIM
SMALL_W = 3 * LANE

NT_DIMS = (((1,), (1,)), ((), ()))


def _cparams(sem):
    return pltpu.CompilerParams(dimension_semantics=sem, vmem_limit_bytes=VMEM_LIMIT)


def _silu(g):
    return g * jax.nn.sigmoid(g)


def _rmsnorm_kernel(x_ref, g_ref, o_ref):
    x = x_ref[...]
    ms = jnp.mean(x * x, axis=-1, keepdims=True)
    o_ref[...] = (x * lax.rsqrt(ms + EPS) * g_ref[...]).astype(o_ref.dtype)


def _rmsnorm(x, g, tm):
    m, d = x.shape
    return pl.pallas_call(
        _rmsnorm_kernel,
        grid=(m // tm,),
        in_specs=[pl.BlockSpec((tm, d), lambda i: (i, 0)),
                  pl.BlockSpec((1, d), lambda i: (0, 0))],
        out_specs=pl.BlockSpec((tm, d), lambda i: (i, 0)),
        out_shape=jax.ShapeDtypeStruct((m, d), BF16),
        compiler_params=_cparams(("parallel",)),
        name="rmsnorm",
    )(x, g.reshape(1, d))


def _rope_chunks(acc, cosf, sinf):
    out = []
    for c in range(acc.shape[1] // LANE):
        ch = acc[:, c * LANE:(c + 1) * LANE]
        out.append(ch * cosf + pltpu.roll(ch, HEAD_DIM // 2, axis=1) * sinf)
    return out


def _store_chunks(chunks, out_refs):
    for c, ch in enumerate(chunks):
        for o in out_refs:
            o[:, c * LANE:(c + 1) * LANE] = ch.astype(o.dtype)


def _proj_kernel(*refs, mode, n_rope_tiles):
    h_ref, w_ref = refs[0], refs[1]
    acc = jnp.dot(h_ref[...], w_ref[...], preferred_element_type=F32)
    if mode == "plain":
        for o in refs[2:]:
            o[...] = acc.astype(o.dtype)
    elif mode == "rope":
        cos_ref, sin_ref = refs[2], refs[3]
        outs = refs[4:]
        j = pl.program_id(1)

        @pl.when(j < n_rope_tiles)
        def _():
            _store_chunks(_rope_chunks(acc, cos_ref[...], sin_ref[...]), outs)

        @pl.when(j >= n_rope_tiles)
        def _():
            for o in outs:
                o[...] = acc.astype(o.dtype)
    else:
        cos_ref, sin_ref, bf_ref, o = refs[2], refs[3], refs[4], refs[5]
        x = acc[:, 0:LANE] + bf_ref[...]
        o[:, 0:LANE] = jnp.minimum(x, 0.0) - jnp.log1p(jnp.exp(-jnp.abs(x)))
        o[:, LANE:2 * LANE] = acc[:, LANE:2 * LANE]
        o[:, 2 * LANE:3 * LANE] = _rope_chunks(acc[:, 2 * LANE:3 * LANE], cos_ref[...], sin_ref[...])[0]


def _proj(h, w, out_dtypes, tm, tn, mode="plain", rope=None, n_rope_cols=None, bias=None):
    m, k = h.shape
    n = w.shape[1]
    grid = (m // tm, n // tn)
    in_specs = [pl.BlockSpec((tm, k), lambda i, j: (i, 0)),
                pl.BlockSpec((k, tn), lambda i, j: (0, j))]
    args = [h, w]
    if mode != "plain":
        cosf, sinf = rope
        nb = cosf.shape[0] // tm
        tab_spec = pl.BlockSpec((tm, LANE), lambda i, j: (i % nb, 0))
        in_specs += [tab_spec, tab_spec]
        args += [cosf, sinf]
    if mode == "small":
        in_specs.append(pl.BlockSpec((1, LANE), lambda i, j: (0, 0)))
        args.append(bias)
    n_rope_tiles = (n if n_rope_cols is None else n_rope_cols) // tn
    outs = pl.pallas_call(
        functools.partial(_proj_kernel, mode=mode, n_rope_tiles=n_rope_tiles),
        grid=grid,
        in_specs=in_specs,
        out_specs=[pl.BlockSpec((tm, tn), lambda i, j: (i, j)) for _ in out_dtypes],
        out_shape=[jax.ShapeDtypeStruct((m, n), dt) for dt in out_dtypes],
        compiler_params=_cparams(("parallel", "parallel")),
        name="proj_" + mode,
    )(*args)
    return outs


def _cumsum_kernel(x_ref, o_ref):
    x = x_ref[...]
    s = x.shape[0]
    row = lax.broadcasted_iota(I32, x.shape, 0)
    k = 1
    while k < s:
        x = x + jnp.where(row >= k, pltpu.roll(x, k, axis=0), 0.0)
        k *= 2
    o_ref[...] = x


def _cumsum_logf(small, batch, seq):
    return pl.pallas_call(
        _cumsum_kernel,
        grid=(batch,),
        in_specs=[pl.BlockSpec((seq, LANE), lambda b: (b, 0))],
        out_specs=pl.BlockSpec((seq, LANE), lambda b: (b, 0)),
        out_shape=jax.ShapeDtypeStruct((batch * seq, LANE), F32),
        compiler_params=_cparams(("parallel",)),
        name="cumsum_logf",
    )(small)


def _attn_prompt_kernel(*refs, mode, n_heads, tq):
    q_ref, k_ref, v_ref, g_ref = refs[:4]
    o_ref = refs[-1]
    seq = k_ref.shape[0]
    if mode == "fox":
        c_ref, ct_ref = refs[4], refs[5]
        qpos = pl.program_id(1) * tq + lax.broadcasted_iota(I32, (tq, seq), 0)
        kpos = lax.broadcasted_iota(I32, (tq, seq), 1)
        causal = kpos <= qpos
    else:
        mask_bias = refs[4][...].astype(F32)
    for h in range(n_heads):
        sl = slice(h * HEAD_DIM, (h + 1) * HEAD_DIM)
        s = lax.dot_general(q_ref[:, sl], k_ref[:, sl], NT_DIMS, preferred_element_type=F32) * SCALE
        if mode == "fox":
            s = s + c_ref[:, h:h + 1] - ct_ref[h:h + 1, :]
            s = jnp.where(causal, s, -jnp.inf)
        else:
            s = s + mask_bias
        m = jnp.max(s, axis=1, keepdims=True)
        p = jnp.exp(s - m)
        l = jnp.sum(p, axis=1, keepdims=True)
        o = jnp.dot(p.astype(BF16), v_ref[:, sl], preferred_element_type=F32)
        o_ref[:, sl] = (o * (1.0 / l) * _silu(g_ref[:, sl])).astype(o_ref.dtype)


def _attn_prompt(q, kv, gate, extra, mode, batch, seq, tq):
    nq = seq // tq
    width = q.shape[1]
    n_heads = width // HEAD_DIM
    row_spec = lambda w: pl.BlockSpec((tq, w), lambda b, i: (b * nq + i, 0))
    in_specs = [row_spec(width),
                pl.BlockSpec((seq, width), lambda b, i: (b, 0)),
                pl.BlockSpec((seq, width), lambda b, i: (b, 1)),
                row_spec(width)]
    args = [q, kv, kv, gate]
    if mode == "fox":
        c, ct = extra
        in_specs += [row_spec(LANE), pl.BlockSpec((None, HEAD_ROWS, seq), lambda b, i: (b, 0, 0))]
        args += [c, ct]
    else:
        in_specs.append(row_spec(seq))
        args.append(extra)
    return pl.pallas_call(
        functools.partial(_attn_prompt_kernel, mode=mode, n_heads=n_heads, tq=tq),
        grid=(batch, nq),
        in_specs=in_specs,
        out_specs=row_spec(width),
        out_shape=jax.ShapeDtypeStruct((batch * seq, width), BF16),
        compiler_params=_cparams(("parallel", "parallel")),
        name="attn_prompt_" + mode,
    )(*args)


def _pool_prompt_kernel(u_ref, g_ref, w_ref, sc_ref, o_ref):
    gi = pl.program_id(1)
    u = u_ref[...]
    row = lax.broadcasted_iota(I32, u.shape, 0)

    def shifted(x, k):
        return jnp.where(row >= k, pltpu.roll(x, k, axis=0), 0.0)

    s2 = u + shifted(u, 1)
    s4 = s2 + shifted(s2, 2)
    s8 = s4 + shifted(s4, 4)
    s16 = s8 + shifted(s8, 8)
    sw = jnp.where(gi == 0, s2, jnp.where(gi == 1, s4, jnp.where(gi == 2, s8, s16)))
    window = jnp.left_shift(jnp.int32(POOL_WINDOWS[0]), gi)
    cnt = jnp.minimum(window, row + 1).astype(F32)
    d = sw / cnt - u
    y = jnp.dot(d.astype(BF16), w_ref[...], preferred_element_type=F32) * sc_ref[...]
    o_ref[...] = (y * _silu(g_ref[...])).astype(o_ref.dtype)


def _pool_prompt(pb, w_pool, pool_scale, batch, seq):
    return pl.pallas_call(
        _pool_prompt_kernel,
        grid=(batch, POOL_GROUPS),
        in_specs=[pl.BlockSpec((seq, POOL_GW), lambda b, g: (b, g)),
                  pl.BlockSpec((seq, POOL_GW), lambda b, g: (b, POOL_GROUPS + g)),
                  pl.BlockSpec((None, POOL_GW, POOL_GW), lambda b, g: (g, 0, 0)),
                  pl.BlockSpec((1, POOL_GW), lambda b, g: (0, g))],
        out_specs=pl.BlockSpec((seq, POOL_GW), lambda b, g: (b, g)),
        out_shape=jax.ShapeDtypeStruct((batch * seq, B_WIDTH), BF16),
        compiler_params=_cparams(("parallel", "parallel")),
        name="pool_prompt",
    )(pb, pb, w_pool, pool_scale)


def _order_key(score):
    score = jnp.where(score == 0.0, 0.0, score)
    bits = lax.bitcast_convert_type(score, I32)
    return bits ^ ((bits >> 31) & jnp.int32(0x7FFFFFFF))


def _select_topk(count, key, idx, k, n_idx_bits):
    kf = float(k)
    t = jnp.where(count(key >= 0) >= kf, jnp.int32(0), jnp.int32(INT_MIN))

    def value_bit(it, t):
        cand = t | jnp.left_shift(jnp.int32(1), 30 - it)
        return jnp.where(count(key >= cand) >= kf, cand, t)

    t = lax.fori_loop(0, 31, value_bit, t)
    gt = key > t
    eq = key == t
    need = kf - count(gt)

    def index_bit(it, j):
        cand = j | jnp.left_shift(jnp.int32(1), n_idx_bits - 1 - it)
        return jnp.where(count(eq & (idx < cand)) < need, cand, j)

    j = lax.fori_loop(0, n_idx_bits, index_bit, jnp.zeros_like(t))
    return gt | (eq & (idx <= j))


def _indexer_prompt_kernel(iq_ref, ik_ref, iw_ref, o_ref, *, tq, topk):
    seq = ik_ref.shape[0]
    ikb = ik_ref[...].astype(BF16)
    score = jnp.zeros((tq, seq), F32)
    for h in range(IDX_HEADS):
        r = lax.dot_general(iq_ref[:, h * IDX_DIM:(h + 1) * IDX_DIM], ikb, NT_DIMS,
                            preferred_element_type=F32)
        score = score + iw_ref[:, h:h + 1] * jnp.maximum(r, 0.0)
    qpos = pl.program_id(1) * tq + lax.broadcasted_iota(I32, (tq, seq), 0)
    kpos = lax.broadcasted_iota(I32, (tq, seq), 1)
    causal = kpos <= qpos
    key = _order_key(jnp.where(causal, score, -jnp.inf))
    count = lambda m: jnp.sum(m.astype(F32), axis=1, keepdims=True)
    sel = _select_topk(count, key, kpos, topk, int(seq).bit_length())
    o_ref[...] = jnp.where(sel & causal, 0.0, -jnp.inf).astype(o_ref.dtype)


def _indexer_prompt(iq, small, batch, seq, tq, topk):
    nq = seq // tq
    return pl.pallas_call(
        functools.partial(_indexer_prompt_kernel, tq=tq, topk=topk),
        grid=(batch, nq),
        in_specs=[pl.BlockSpec((tq, IDX_HEADS * IDX_DIM), lambda b, i: (b * nq + i, 0)),
                  pl.BlockSpec((seq, LANE), lambda b, i: (b, 2)),
                  pl.BlockSpec((tq, LANE), lambda b, i: (b * nq + i, 1))],
        out_specs=pl.BlockSpec((tq, seq), lambda b, i: (b * nq + i, 0)),
        out_shape=jax.ShapeDtypeStruct((batch * seq, seq), BF16),
        compiler_params=_cparams(("parallel", "parallel")),
        name="indexer_prompt",
    )(iq, small, small)


def _outproj_kernel(a_ref, b_ref, c_ref, wa_ref, wb_ref, wc_ref, y_ref):
    y = jnp.dot(a_ref[...], wa_ref[...], preferred_element_type=F32)
    y = y + jnp.dot(b_ref[...], wb_ref[...], preferred_element_type=F32)
    y = y + jnp.dot(c_ref[...], wc_ref[...], preferred_element_type=F32)
    y_ref[...] = y


def _outproj(a, b, c, wa, wb, wc, tm, tn):
    m = a.shape[0]
    n = wa.shape[1]
    lhs = lambda w: pl.BlockSpec((tm, w), lambda i, j: (i, 0))
    rhs = lambda w: pl.BlockSpec((w, tn), lambda i, j: (0, j))
    return pl.pallas_call(
        _outproj_kernel,
        grid=(m // tm, n // tn),
        in_specs=[lhs(A_WIDTH), lhs(B_WIDTH), lhs(C_WIDTH), rhs(A_WIDTH), rhs(B_WIDTH), rhs(C_WIDTH)],
        out_specs=pl.BlockSpec((tm, tn), lambda i, j: (i, j)),
        out_shape=jax.ShapeDtypeStruct((m, n), F32),
        compiler_params=_cparams(("parallel", "parallel")),
        name="outproj",
    )(a, b, c, wa, wb, wc)


def _postnorm_kernel(x_ref, y_ref, g_ref, o_ref):
    y = y_ref[...]
    ms = jnp.mean(y * y, axis=-1, keepdims=True)
    o_ref[...] = x_ref[...] + y * lax.rsqrt(ms + EPS) * g_ref[...]


def _postnorm(x, y, g, tm):
    m, d = x.shape
    spec = pl.BlockSpec((tm, d), lambda i: (i, 0))
    return pl.pallas_call(
        _postnorm_kernel,
        grid=(m // tm,),
        in_specs=[spec, spec, pl.BlockSpec((1, d), lambda i: (0, 0))],
        out_specs=spec,
        out_shape=jax.ShapeDtypeStruct((m, d), F32),
        compiler_params=_cparams(("parallel",)),
        name="postnorm",
    )(x, y, g.reshape(1, d))


def _logf_bias_kernel(pt_ref, lf_ref, init_ref, o_ref, carry_ref):
    @pl.when(pl.program_id(1) == 0)
    def _():
        carry_ref[...] = init_ref[...]

    x = lf_ref[...]
    lane = lax.broadcasted_iota(I32, x.shape, 1)
    s = x
    k = 1
    while k < PAGE_SIZE:
        s = s + jnp.where(lane + k < PAGE_SIZE, pltpu.roll(s, PAGE_SIZE - k, axis=1), 0.0)
        k *= 2
    carry = carry_ref[...]
    o_ref[...] = carry + (s - x)
    carry_ref[...] = carry + jnp.sum(x, axis=1, keepdims=True)


def _logf_bias(page_table, logf_t, layer, init):
    bd, n_pages = page_table.shape
    grid_spec = pltpu.PrefetchScalarGridSpec(
        num_scalar_prefetch=1,
        grid=(bd, n_pages),
        in_specs=[pl.BlockSpec((None, None, HEAD_ROWS, PAGE_SIZE),
                               lambda b, j, pt: (layer, pt[b, n_pages - 1 - j], 0, 0)),
                  pl.BlockSpec((None, HEAD_ROWS, LANE), lambda b, j, pt: (b, 0, 0))],
        out_specs=pl.BlockSpec((None, None, HEAD_ROWS, PAGE_SIZE),
                               lambda b, j, pt: (b, n_pages - 1 - j, 0, 0)),
        scratch_shapes=[pltpu.VMEM((HEAD_ROWS, LANE), F32)],
    )
    return pl.pallas_call(
        _logf_bias_kernel,
        grid_spec=grid_spec,
        out_shape=jax.ShapeDtypeStruct((bd, n_pages, HEAD_ROWS, PAGE_SIZE), F32),
        compiler_params=_cparams(("parallel", "arbitrary")),
        name="logf_bias_sample",
    )(page_table, logf_t, init)


def _indexer_sample_kernel(pt_ref, ik_ref, iq_ref, iw_ref, ikn_ref, mb_ref, nb_ref, sc_ref, *, topk, n_pages):
    j = pl.program_id(1)
    iq = iq_ref[...]
    iw = iw_ref[...]
    r = lax.dot_general(iq, ik_ref[...].astype(BF16), NT_DIMS, preferred_element_type=F32)
    sc_ref[pl.ds(j, 1), :] = jnp.sum(iw * jnp.maximum(r, 0.0), axis=0, keepdims=True)

    @pl.when(j == n_pages - 1)
    def _():
        rn = jnp.sum(iq.astype(F32) * ikn_ref[...].astype(F32), axis=1, keepdims=True)
        sn = jnp.sum(iw[:, 0:1] * jnp.maximum(rn, 0.0), axis=0, keepdims=True)
        key = _order_key(sc_ref[...])
        key_n = _order_key(sn)
        shape = key.shape
        idx = lax.broadcasted_iota(I32, shape, 0) * PAGE_SIZE + lax.broadcasted_iota(I32, shape, 1)
        idx_n = jnp.full((1, 1), n_pages * PAGE_SIZE, I32)

        def total(m):
            return jnp.sum(jnp.sum(m.astype(F32), axis=1, keepdims=True), axis=0, keepdims=True)

        keys = (key, key_n)
        idxs = (idx, idx_n)
        kf = float(topk)
        cnt_ge = lambda c: total(keys[0] >= c) + total(keys[1] >= c)
        t = jnp.where(cnt_ge(0) >= kf, jnp.int32(0), jnp.int32(INT_MIN))

        def value_bit(it, t):
            cand = t | jnp.left_shift(jnp.int32(1), 30 - it)
            return jnp.where(cnt_ge(cand) >= kf, cand, t)

        t = lax.fori_loop(0, 31, value_bit, t)
        need = kf - (total(keys[0] > t) + total(keys[1] > t))
        n_bits = int(n_pages * PAGE_SIZE + 1).bit_length()

        def index_bit(it, jj):
            cand = jj | jnp.left_shift(jnp.int32(1), n_bits - 1 - it)
            f = total((keys[0] == t) & (idxs[0] < cand)) + total((keys[1] == t) & (idxs[1] < cand))
            return jnp.where(f < need, cand, jj)

        jj = lax.fori_loop(0, n_bits, index_bit, jnp.zeros_like(t))
        sel = (key > t) | ((key == t) & (idx <= jj))
        sel_n = (key_n > t) | ((key_n == t) & (idx_n <= jj))
        mb_ref[...] = jnp.where(sel, 0.0, -jnp.inf)
        nb_ref[...] = jnp.broadcast_to(jnp.where(sel_n, 0.0, -jnp.inf), nb_ref.shape)


def _indexer_sample(page_table, cache_idxk, layer, iq, iw, ik_new, topk):
    bd, n_pages = page_table.shape
    grid_spec = pltpu.PrefetchScalarGridSpec(
        num_scalar_prefetch=1,
        grid=(bd, n_pages),
        in_specs=[pl.BlockSpec((None, None, PAGE_SIZE, IDX_DIM), lambda b, j, pt: (layer, pt[b, j], 0, 0)),
                  pl.BlockSpec((None, IDX_HEADS, IDX_DIM), lambda b, j, pt: (b, 0, 0)),
                  pl.BlockSpec((None, IDX_HEADS, LANE), lambda b, j, pt: (b, 0, 0)),
                  pl.BlockSpec((None, 1, IDX_DIM), lambda b, j, pt: (b, 0, 0))],
        out_specs=[pl.BlockSpec((None, n_pages, PAGE_SIZE), lambda b, j, pt: (b, 0, 0)),
                   pl.BlockSpec((None, 8, LANE), lambda b, j, pt: (b, 0, 0))],
        scratch_shapes=[pltpu.VMEM((n_pages, PAGE_SIZE), F32)],
    )
    return pl.pallas_call(
        functools.partial(_indexer_sample_kernel, topk=topk, n_pages=n_pages),
        grid_spec=grid_spec,
        out_shape=[jax.ShapeDtypeStruct((bd, n_pages, PAGE_SIZE), F32),
                   jax.ShapeDtypeStruct((bd, 8, LANE), F32)],
        compiler_params=_cparams(("parallel", "arbitrary")),
        name="indexer_sample",
    )(page_table, cache_idxk, iq, iw, ik_new)


def _attn_sample_kernel(pt_ref, q_ref, kv_ref, bias_ref, nkv_ref, nbias_ref, g_ref, o_ref,
                        m_ref, l_ref, acc_ref, *, n_pages, width):
    j = pl.program_id(1)

    @pl.when(j == 0)
    def _():
        m_ref[...] = jnp.full(m_ref.shape, NEG_BIG, F32)
        l_ref[...] = jnp.zeros(l_ref.shape, F32)
        acc_ref[...] = jnp.zeros(acc_ref.shape, F32)

    def step(page_ref, b_ref):
        kb = page_ref[:, 0:width].astype(BF16)
        vb = page_ref[:, width:2 * width].astype(BF16)
        s = lax.dot_general(q_ref[...], kb, NT_DIMS, preferred_element_type=F32) * SCALE + b_ref[...]
        m_old = m_ref[...]
        m_new = jnp.maximum(m_old, jnp.max(s, axis=1, keepdims=True))
        alpha = jnp.exp(m_old - m_new)
        p = jnp.exp(s - m_new)
        l_ref[...] = alpha * l_ref[...] + jnp.sum(p, axis=1, keepdims=True)
        acc_ref[...] = alpha[:, 0:1] * acc_ref[...] + jnp.dot(p.astype(BF16), vb, preferred_element_type=F32)
        m_ref[...] = m_new

    @pl.when(j < n_pages)
    def _():
        step(kv_ref, bias_ref)

    @pl.when(j == n_pages)
    def _():
        step(nkv_ref, nbias_ref)
        acc = acc_ref[...] / l_ref[:, 0:1]
        row = lax.broadcasted_iota(I32, acc.shape, 0)
        col_head = lax.broadcasted_iota(I32, acc.shape, 1) // HEAD_DIM
        own = jnp.sum(jnp.where(row == col_head, acc, 0.0), axis=0, keepdims=True)
        o_ref[...] = own * _silu(g_ref[...])


def _attn_sample(page_table, q_bd, cache_kv, layer, bias, new_kv, new_bias, gate):
    bd, n_pages = page_table.shape
    width = q_bd.shape[2]
    bias_rows = bias.shape[2]
    last = n_pages - 1
    grid_spec = pltpu.PrefetchScalarGridSpec(
        num_scalar_prefetch=1,
        grid=(bd, n_pages + 1),
        in_specs=[pl.BlockSpec((None, HEAD_ROWS, width), lambda b, j, pt: (b, 0, 0)),
                  pl.BlockSpec((None, None, PAGE_SIZE, 2 * width),
                               lambda b, j, pt: (layer, pt[b, jnp.minimum(j, last)], 0, 0)),
                  pl.BlockSpec((None, None, bias_rows, PAGE_SIZE),
                               lambda b, j, pt: (b, jnp.minimum(j, last), 0, 0)),
                  pl.BlockSpec((None, PAGE_SIZE, 2 * width), lambda b, j, pt: (b, 0, 0)),
                  pl.BlockSpec((None, 1, PAGE_SIZE), lambda b, j, pt: (b, 0, 0)),
                  pl.BlockSpec((None, 1, width), lambda b, j, pt: (b, 0, 0))],
        out_specs=pl.BlockSpec((None, 1, width), lambda b, j, pt: (b, 0, 0)),
        scratch_shapes=[pltpu.VMEM((HEAD_ROWS, LANE), F32),
                        pltpu.VMEM((HEAD_ROWS, LANE), F32),
                        pltpu.VMEM((HEAD_ROWS, width), F32)],
    )
    return pl.pallas_call(
        functools.partial(_attn_sample_kernel, n_pages=n_pages, width=width),
        grid_spec=grid_spec,
        out_shape=jax.ShapeDtypeStruct((bd, 1, width), F32),
        compiler_params=_cparams(("parallel", "arbitrary")),
        name="attn_sample",
    )(page_table, q_bd, cache_kv, bias, new_kv, new_bias, gate)


def _pool_sample_kernel(ext_ref, g_ref, w_ref, sc_ref, o_ref, *, past_len):
    hist = POOL_HIST + 1
    for g, w in enumerate(POOL_WINDOWS):
        sl = slice(g * POOL_GW, (g + 1) * POOL_GW)
        sw = jnp.sum(ext_ref[hist - w:hist, :, sl], axis=0)
        u = ext_ref[hist - 1, :, sl]
        d = sw / float(min(w, past_len + 1)) - u
        y = jnp.dot(d.astype(BF16), w_ref[g], preferred_element_type=F32) * sc_ref[:, sl]
        o_ref[:, sl] = y * _silu(g_ref[:, sl])


def _pool_sample(ext_t, gate, w_pool, pool_scale, past_len):
    rows = ext_t.shape[1]
    return pl.pallas_call(
        functools.partial(_pool_sample_kernel, past_len=past_len),
        out_shape=jax.ShapeDtypeStruct((rows, B_WIDTH), F32),
        compiler_params=pltpu.CompilerParams(vmem_limit_bytes=VMEM_LIMIT),
        name="pool_sample",
    )(ext_t, gate, w_pool, pool_scale)


def _rope_tables(pos):
    half = HEAD_DIM // 2
    inv = jnp.power(ROPE_THETA, -jnp.arange(half, dtype=F32) / half)
    ang = pos.astype(F32)[:, None] * inv[None, :]
    cos, sin = jnp.cos(ang), jnp.sin(ang)
    return jnp.concatenate([cos, cos], axis=1), jnp.concatenate([-sin, sin], axis=1)


def _prep_weights(w_in, b_f, w_pool, w_out):
    col = lambda a, n: w_in[:, a:a + n].astype(BF16)
    d = w_in.shape[0]
    small = jnp.concatenate(
        [w_in[:, O_AF:O_AF + A_HEADS], jnp.zeros((d, LANE - A_HEADS), F32),
         w_in[:, O_IW:O_IW + IDX_HEADS], jnp.zeros((d, LANE - IDX_HEADS), F32),
         w_in[:, O_IK:O_IK + IDX_DIM]], axis=1).astype(BF16)
    return dict(
        aq=col(O_AQ, A_WIDTH), akv=col(O_AK, 2 * A_WIDTH), ag=col(O_AG, A_WIDTH),
        b=col(O_BU, 2 * B_WIDTH),
        cq=col(O_CQ, C_WIDTH), ckv=col(O_CK, 2 * C_WIDTH), cg=col(O_CG, C_WIDTH),
        iq=col(O_IQ, IDX_HEADS * IDX_DIM), small=small,
        bf=jnp.pad(b_f, (0, LANE - A_HEADS)).reshape(1, LANE),
        pool=w_pool.astype(BF16),
        oa=w_out[0:A_WIDTH].astype(BF16),
        ob=w_out[A_WIDTH:A_WIDTH + B_WIDTH].astype(BF16),
        oc=w_out[A_WIDTH + B_WIDTH:].astype(BF16),
    )


def _project_all(h, w, rope, tm, tn):
    p = {}
    (p["aq"],) = _proj(h, w["aq"], [BF16], tm, tn)
    p["akv"], p["akv_b"] = _proj(h, w["akv"], [F32, BF16], tm, tn)
    (p["ag"],) = _proj(h, w["ag"], [F32], tm, tn)
    (p["b"],) = _proj(h, w["b"], [F32], tm, tn)
    (p["cq"],) = _proj(h, w["cq"], [BF16], tm, tn, mode="rope", rope=rope)
    p["ckv"], p["ckv_b"] = _proj(h, w["ckv"], [F32, BF16], tm, tn, mode="rope", rope=rope, n_rope_cols=C_WIDTH)
    (p["cg"],) = _proj(h, w["cg"], [F32], tm, tn)
    (p["iq"],) = _proj(h, w["iq"], [BF16], tm, tn, mode="rope", rope=rope)
    (p["small"],) = _proj(h, w["small"], [F32], tm, SMALL_W, mode="small", rope=rope, bias=w["bf"])
    return p


def _block_diag_q(q, n_heads):
    rows = q.shape[0]
    eye = jnp.eye(HEAD_ROWS, n_heads, dtype=q.dtype)
    qh = q.reshape(rows, 1, n_heads, HEAD_DIM) * eye[None, :, :, None]
    return qh.reshape(rows, HEAD_ROWS, n_heads * HEAD_DIM)


def _new_token_page(kv_new):
    return jnp.pad(kv_new[:, None, :], ((0, 0), (0, PAGE_SIZE - 1), (0, 0)))


def kernel(x_prompt, x_sample, cache_a_kv, cache_a_logf, cache_c_kv, cache_c_idxk, state_pool,
           page_table, g_pre, w_in, b_f, w_pool, pool_scale, w_out, g_post):
    batch, seq, d_model = x_prompt.shape
    bd = x_sample.shape[0]
    depth = g_pre.shape[0]
    n_pool = cache_a_kv.shape[1]
    n_pages = page_table.shape[1]
    past_len = n_pages * PAGE_SIZE
    m_p = batch * seq
    tm_p = min(1024, m_p)
    tn = 512
    tq = min(256, seq)
    topk_p = min(TOPK_MAX, seq // 4)
    topk_s = min(TOPK_MAX, (past_len + 1) // 4)

    rope_p = _rope_tables(jnp.arange(seq))
    rope_s = _rope_tables(jnp.full((SAMPLE_ROWS,), past_len))
    lane0 = lax.broadcasted_iota(I32, (bd, 1, PAGE_SIZE), 2) == 0

    cache_a = cache_a_kv.reshape(depth, n_pool, PAGE_SIZE, 2 * A_WIDTH)
    cache_c = cache_c_kv.reshape(depth, n_pool, PAGE_SIZE, 2 * C_WIDTH)
    logf_t = jnp.pad(jnp.swapaxes(cache_a_logf, 2, 3), ((0, 0), (0, 0), (0, HEAD_ROWS - A_HEADS), (0, 0)))

    xp = x_prompt.reshape(m_p, d_model)
    xs = jnp.pad(x_sample.reshape(bd, d_model), ((0, SAMPLE_ROWS - bd), (0, 0)))

    outs = {k: [] for k in ("p_akv", "p_alogf", "p_ckv", "p_cik", "p_pool",
                            "s_akv", "s_alogf", "s_ckv", "s_cik", "s_pool")}
    for l in range(depth):
        w = _prep_weights(w_in[l], b_f[l], w_pool[l], w_out[l])
        scale_row = pool_scale[l].reshape(1, B_WIDTH)

        h = _rmsnorm(xp, g_pre[l], 256)
        p = _project_all(h, w, rope_p, tm_p, tn)
        c = _cumsum_logf(p["small"], batch, seq)
        c_t = jnp.pad(jnp.swapaxes(c.reshape(batch, seq, LANE)[:, :, :A_HEADS], 1, 2),
                      ((0, 0), (0, HEAD_ROWS - A_HEADS), (0, 0)))
        mix_a = _attn_prompt(p["aq"], p["akv_b"], p["ag"], (c, c_t), "fox", batch, seq, tq)
        mix_b = _pool_prompt(p["b"], w["pool"], scale_row, batch, seq)
        mask_bias = _indexer_prompt(p["iq"], p["small"], batch, seq, tq, topk_p)
        mix_c = _attn_prompt(p["cq"], p["ckv_b"], p["cg"], mask_bias, "dsa", batch, seq, tq)
        y = _outproj(mix_a, mix_b, mix_c, w["oa"], w["ob"], w["oc"], tm_p, tn)
        xp = _postnorm(xp, y, g_post[l], 256)

        outs["p_akv"].append(p["akv"].reshape(batch, seq, 2, A_HEADS, HEAD_DIM))
        outs["p_alogf"].append(p["small"][:, :A_HEADS].reshape(batch, seq, A_HEADS))
        outs["p_ckv"].append(p["ckv"].reshape(batch, seq, 2, C_HEADS, HEAD_DIM))
        outs["p_cik"].append(p["small"][:, 2 * LANE:].reshape(batch, seq, IDX_DIM))
        outs["p_pool"].append(p["b"].reshape(batch, seq, 2 * B_WIDTH)[:, seq - POOL_HIST:, :B_WIDTH])

        hs = _rmsnorm(xs, g_pre[l], SAMPLE_ROWS)
        s = _project_all(hs, w, rope_s, SAMPLE_ROWS, tn)
        logf_s = s["small"][:bd, :A_HEADS]
        init = jnp.broadcast_to(jnp.pad(logf_s, ((0, 0), (0, HEAD_ROWS - A_HEADS)))[:, :, None],
                                (bd, HEAD_ROWS, LANE))
        fox_bias = _logf_bias(page_table, logf_t, l, init)
        new_bias_a = jnp.where(lane0, 0.0, -jnp.inf).astype(F32)
        sa = _attn_sample(page_table, _block_diag_q(s["aq"][:bd], A_HEADS), cache_a, l, fox_bias,
                          _new_token_page(s["akv"][:bd]), new_bias_a, s["ag"][:bd, None, :])

        iq_s = s["iq"][:bd].reshape(bd, IDX_HEADS, IDX_DIM)
        iw_s = jnp.broadcast_to(s["small"][:bd, LANE:LANE + IDX_HEADS, None], (bd, IDX_HEADS, LANE))
        ikn_s = s["small"][:bd, None, 2 * LANE:].astype(BF16)
        dsa_bias, nb = _indexer_sample(page_table, cache_c_idxk, l, iq_s, iw_s, ikn_s, topk_s)
        new_bias_c = jnp.where(lane0, nb[:, 0:1, :], -jnp.inf)
        sc = _attn_sample(page_table, _block_diag_q(s["cq"][:bd], C_HEADS), cache_c, l,
                          dsa_bias.reshape(bd, n_pages, 1, PAGE_SIZE),
                          _new_token_page(s["ckv"][:bd]), new_bias_c, s["cg"][:bd, None, :])

        bu_s = s["b"][:bd, :B_WIDTH]
        ext = jnp.concatenate([state_pool[l], bu_s[:, None, :]], axis=1)
        ext_t = jnp.pad(jnp.swapaxes(ext, 0, 1), ((0, 0), (0, SAMPLE_ROWS - bd), (0, 0)))
        sb = _pool_sample(ext_t, s["b"][:, B_WIDTH:], w["pool"], scale_row, past_len)

        pad_rows = lambda t: jnp.pad(t, ((0, SAMPLE_ROWS - bd), (0, 0))).astype(BF16)
        ys = _outproj(pad_rows(sa[:, 0, :]), sb.astype(BF16), pad_rows(sc[:, 0, :]),
                      w["oa"], w["ob"], w["oc"], SAMPLE_ROWS, tn)
        xs = _postnorm(xs, ys, g_post[l], SAMPLE_ROWS)

        outs["s_akv"].append(s["akv"][:bd].reshape(bd, 1, 2, A_HEADS, HEAD_DIM))
        outs["s_alogf"].append(logf_s.reshape(bd, 1, A_HEADS))
        outs["s_ckv"].append(s["ckv"][:bd].reshape(bd, 1, 2, C_HEADS, HEAD_DIM))
        outs["s_cik"].append(s["small"][:bd, 2 * LANE:].reshape(bd, 1, IDX_DIM))
        outs["s_pool"].append(ext[:, 1:, :])

    st = lambda k: jnp.stack(outs[k])
    return (xp.reshape(batch, seq, d_model), xs[:bd].reshape(bd, 1, d_model),
            st("p_akv"), st("p_alogf"), st("p_ckv"), st("p_cik"), st("p_pool"),
            st("s_akv"), st("s_alogf"), st("s_ckv"), st("s_cik"), st("s_pool"))
```

```python
import functools

import numpy as np
import jax
import jax.numpy as jnp
from jax import lax
from jax.experimental import pallas as pl
from jax.experimental.pallas import tpu as pltpu

F32 = jnp.float32
BF16 = jnp.bfloat16
I32 = jnp.int32

D_MODEL = 4096
HEAD_DIM = 128
A_HEADS = 12
A_WIDTH = A_HEADS * HEAD_DIM
C_HEADS = 12
C_WIDTH = C_HEADS * HEAD_DIM
B_WIDTH = D_MODEL - A_WIDTH - C_WIDTH
POOL_WINDOWS = (2, 4, 8, 16)
POOL_GROUPS = len(POOL_WINDOWS)
POOL_GW = B_WIDTH // POOL_GROUPS
POOL_HIST = max(POOL_WINDOWS) - 1
IDX_HEADS = 32
IDX_DIM = 128
TOPK_MAX = 256
ROPE_THETA = 10000.0
EPS = 1e-6
PAGE_SIZE = 128
SCALE = HEAD_DIM ** -0.5

LANE = 128
HEAD_ROWS = 16
SAMPLE_ROWS = 16
SAMPLE_PAGES_PER_STEP = 8
KV_TN = 8 * LANE
VMEM_LIMIT = 56 * 1024 * 1024
NEG_BIG = -1e30
INT_MIN = -2 ** 31

O_AQ = 0
O_AK = O_AQ + A_WIDTH
O_AV = O_AK + A_WIDTH
O_AG = O_AV + A_WIDTH
O_AF = O_AG + A_WIDTH
O_BU = O_AF + A_HEADS
O_BG = O_BU + B_WIDTH
O_CQ = O_BG + B_WIDTH
O_CK = O_CQ + C_WIDTH
O_CV = O_CK + C_WIDTH
O_CG = O_CV + C_WIDTH
O_IQ = O_CG + C_WIDTH
O_IK = O_IQ + IDX_HEADS * IDX_DIM
O_IW = O_IK + IDX_DIM
SMALL_W = 3 * LANE

NT_DIMS = (((1,), (1,)), ((), ()))


def _cparams(sem):
    return pltpu.CompilerParams(dimension_semantics=sem, vmem_limit_bytes=VMEM_LIMIT)


def _silu(g):
    return g * jax.nn.sigmoid(g)


def _rmsnorm_kernel(x_ref, g_ref, o_ref):
    x = x_ref[...]
    ms = jnp.mean(x * x, axis=-1, keepdims=True)
    o_ref[...] = (x * lax.rsqrt(ms + EPS) * g_ref[...]).astype(o_ref.dtype)


def _rmsnorm(x, g, tm):
    m, d = x.shape
    return pl.pallas_call(
        _rmsnorm_kernel,
        grid=(m // tm,),
        in_specs=[pl.BlockSpec((tm, d), lambda i: (i, 0)),
                  pl.BlockSpec((1, d), lambda i: (0, 0))],
        out_specs=pl.BlockSpec((tm, d), lambda i: (i, 0)),
        out_shape=jax.ShapeDtypeStruct((m, d), BF16),
        compiler_params=_cparams(("parallel",)),
        name="rmsnorm",
    )(x, g.reshape(1, d))


def _rope_chunks(acc, cosf, sinf):
    out = []
    for c in range(acc.shape[1] // LANE):
        ch = acc[:, c * LANE:(c + 1) * LANE]
        out.append(ch * cosf + pltpu.roll(ch, HEAD_DIM // 2, axis=1) * sinf)
    return out


def _store_chunks(chunks, out_refs):
    for c, ch in enumerate(chunks):
        for o in out_refs:
            o[:, c * LANE:(c + 1) * LANE] = ch.astype(o.dtype)


def _proj_kernel(*refs, mode, n_rope_tiles):
    h_ref, w_ref = refs[0], refs[1]
    acc = jnp.dot(h_ref[...], w_ref[...], preferred_element_type=F32)
    if mode == "plain":
        for o in refs[2:]:
            o[...] = acc.astype(o.dtype)
    elif mode == "rope":
        cos_ref, sin_ref = refs[2], refs[3]
        outs = refs[4:]
        j = pl.program_id(1)

        @pl.when(j < n_rope_tiles)
        def _():
            _store_chunks(_rope_chunks(acc, cos_ref[...], sin_ref[...]), outs)

        @pl.when(j >= n_rope_tiles)
        def _():
            for o in outs:
                o[...] = acc.astype(o.dtype)
    elif mode in ("kv", "kv_rope"):
        if mode == "kv_rope":
            cos_ref, sin_ref = refs[2], refs[3]
            o3_ref, ob_ref = refs[4], refs[5]
        else:
            o3_ref, ob_ref = refs[2], refs[3]
        for c in range(acc.shape[1] // LANE):
            ch = acc[:, c * LANE:(c + 1) * LANE]
            if mode == "kv_rope" and c % 2 == 0:
                ch = ch * cos_ref[...] + pltpu.roll(ch, HEAD_DIM // 2, axis=1) * sin_ref[...]
            o3_ref[:, c, :] = ch
            ob_ref[:, c * LANE:(c + 1) * LANE] = ch.astype(ob_ref.dtype)
    else:
        cos_ref, sin_ref, bf_ref, o = refs[2], refs[3], refs[4], refs[5]
        x = acc[:, 0:LANE] + bf_ref[...]
        o[:, 0:LANE] = jnp.minimum(x, 0.0) - jnp.log1p(jnp.exp(-jnp.abs(x)))
        o[:, LANE:2 * LANE] = acc[:, LANE:2 * LANE]
        o[:, 2 * LANE:3 * LANE] = _rope_chunks(acc[:, 2 * LANE:3 * LANE], cos_ref[...], sin_ref[...])[0]


def _proj(h, w, out_dtypes, tm, tn, mode="plain", rope=None, n_rope_cols=None, bias=None):
    m, k = h.shape
    n = w.shape[1]
    grid = (m // tm, n // tn)
    in_specs = [pl.BlockSpec((tm, k), lambda i, j: (i, 0)),
                pl.BlockSpec((k, tn), lambda i, j: (0, j))]
    args = [h, w]
    if mode not in ("plain", "kv"):
        cosf, sinf = rope
        nb = cosf.shape[0] // tm
        tab_spec = pl.BlockSpec((tm, LANE), lambda i, j: (i % nb, 0))
        in_specs += [tab_spec, tab_spec]
        args += [cosf, sinf]
    if mode == "small":
        in_specs.append(pl.BlockSpec((1, LANE), lambda i, j: (0, 0)))
        args.append(bias)
    n_rope_tiles = (n if n_rope_cols is None else n_rope_cols) // tn
    out_specs = [pl.BlockSpec((tm, tn), lambda i, j: (i, j)) for _ in out_dtypes]
    out_shape = [jax.ShapeDtypeStruct((m, n), dt) for dt in out_dtypes]
    if mode in ("kv", "kv_rope"):
        out_specs[0] = pl.BlockSpec((tm, tn // LANE, LANE), lambda i, j: (i, j, 0))
        out_shape[0] = jax.ShapeDtypeStruct((m, n // LANE, LANE), out_dtypes[0])
    outs = pl.pallas_call(
        functools.partial(_proj_kernel, mode=mode, n_rope_tiles=n_rope_tiles),
        grid=grid,
        in_specs=in_specs,
        out_specs=out_specs,
        out_shape=out_shape,
        compiler_params=_cparams(("parallel", "parallel")),
        name="proj_" + mode,
    )(*args)
    return outs


def _cumsum_kernel(x_ref, o_ref):
    x = x_ref[...]
    s = x.shape[0]
    row = lax.broadcasted_iota(I32, x.shape, 0)
    k = 1
    while k < s:
        x = x + jnp.where(row >= k, pltpu.roll(x, k, axis=0), 0.0)
        k *= 2
    o_ref[...] = x


def _cumsum_logf(small, batch, seq):
    return pl.pallas_call(
        _cumsum_kernel,
        grid=(batch,),
        in_specs=[pl.BlockSpec((seq, LANE), lambda b: (b, 0))],
        out_specs=pl.BlockSpec((seq, LANE), lambda b: (b, 0)),
        out_shape=jax.ShapeDtypeStruct((batch * seq, LANE), F32),
        compiler_params=_cparams(("parallel",)),
        name="cumsum_logf",
    )(small)


def _attn_prompt_kernel(*refs, mode, n_heads, tq):
    q_ref, kv_ref, g_ref = refs[:3]
    o_ref = refs[-1]
    seq = kv_ref.shape[0]
    if mode == "fox":
        c_ref, ct_ref = refs[3], refs[4]
        qpos = pl.program_id(1) * tq + lax.broadcasted_iota(I32, (tq, seq), 0)
        kpos = lax.broadcasted_iota(I32, (tq, seq), 1)
        causal = kpos <= qpos
    else:
        mask_bias = refs[3][...].astype(F32)
    for h in range(n_heads):
        sl = slice(h * HEAD_DIM, (h + 1) * HEAD_DIM)
        k_sl = slice(2 * h * HEAD_DIM, (2 * h + 1) * HEAD_DIM)
        v_sl = slice((2 * h + 1) * HEAD_DIM, (2 * h + 2) * HEAD_DIM)
        s = lax.dot_general(q_ref[:, sl], kv_ref[:, k_sl], NT_DIMS, preferred_element_type=F32) * SCALE
        if mode == "fox":
            s = s + c_ref[:, h:h + 1] - ct_ref[h:h + 1, :]
            s = jnp.where(causal, s, -jnp.inf)
        else:
            s = s + mask_bias
        m = jnp.max(s, axis=1, keepdims=True)
        p = jnp.exp(s - m)
        l = jnp.sum(p, axis=1, keepdims=True)
        o = jnp.dot(p.astype(BF16), kv_ref[:, v_sl], preferred_element_type=F32)
        o_ref[:, sl] = (o * (1.0 / l) * _silu(g_ref[:, sl])).astype(o_ref.dtype)


def _attn_prompt(q, kv, gate, extra, mode, batch, seq, tq):
    nq = seq // tq
    width = q.shape[1]
    n_heads = width // HEAD_DIM
    row_spec = lambda w: pl.BlockSpec((tq, w), lambda b, i: (b * nq + i, 0))
    in_specs = [row_spec(width),
                pl.BlockSpec((seq, 2 * width), lambda b, i: (b, 0)),
                row_spec(width)]
    args = [q, kv, gate]
    if mode == "fox":
        c, ct = extra
        in_specs += [row_spec(LANE), pl.BlockSpec((None, HEAD_ROWS, seq), lambda b, i: (b, 0, 0))]
        args += [c, ct]
    else:
        in_specs.append(row_spec(seq))
        args.append(extra)
    return pl.pallas_call(
        functools.partial(_attn_prompt_kernel, mode=mode, n_heads=n_heads, tq=tq),
        grid=(batch, nq),
        in_specs=in_specs,
        out_specs=row_spec(width),
        out_shape=jax.ShapeDtypeStruct((batch * seq, width), BF16),
        compiler_params=_cparams(("parallel", "parallel")),
        name="attn_prompt_" + mode,
    )(*args)


def _pool_prompt_kernel(u_ref, g_ref, w_ref, sc_ref, o_ref):
    gi = pl.program_id(1)
    u = u_ref[...]
    row = lax.broadcasted_iota(I32, u.shape, 0)

    def shifted(x, k):
        return jnp.where(row >= k, pltpu.roll(x, k, axis=0), 0.0)

    s2 = u + shifted(u, 1)
    s4 = s2 + shifted(s2, 2)
    s8 = s4 + shifted(s4, 4)
    s16 = s8 + shifted(s8, 8)
    sw = jnp.where(gi == 0, s2, jnp.where(gi == 1, s4, jnp.where(gi == 2, s8, s16)))
    window = jnp.left_shift(jnp.int32(POOL_WINDOWS[0]), gi)
    cnt = jnp.minimum(window, row + 1).astype(F32)
    d = sw / cnt - u
    y = jnp.dot(d.astype(BF16), w_ref[...], preferred_element_type=F32) * sc_ref[...]
    o_ref[...] = (y * _silu(g_ref[...])).astype(o_ref.dtype)


def _pool_prompt(pb, w_pool, pool_scale, batch, seq):
    return pl.pallas_call(
        _pool_prompt_kernel,
        grid=(batch, POOL_GROUPS),
        in_specs=[pl.BlockSpec((seq, POOL_GW), lambda b, g: (b, g)),
                  pl.BlockSpec((seq, POOL_GW), lambda b, g: (b, POOL_GROUPS + g)),
                  pl.BlockSpec((None, POOL_GW, POOL_GW), lambda b, g: (g, 0, 0)),
                  pl.BlockSpec((1, POOL_GW), lambda b, g: (0, g))],
        out_specs=pl.BlockSpec((seq, POOL_GW), lambda b, g: (b, g)),
        out_shape=jax.ShapeDtypeStruct((batch * seq, B_WIDTH), BF16),
        compiler_params=_cparams(("parallel", "parallel")),
        name="pool_prompt",
    )(pb, pb, w_pool, pool_scale)


def _order_key(score):
    score = jnp.where(score == 0.0, 0.0, score)
    bits = lax.bitcast_convert_type(score, I32)
    return bits ^ ((bits >> 31) & jnp.int32(0x7FFFFFFF))


def _select_topk(count, key, idx, k, n_idx_bits):
    kf = float(k)
    t = jnp.where(count(key >= 0) >= kf, jnp.int32(0), jnp.int32(INT_MIN))

    def value_bit(it, t):
        cand = t | jnp.left_shift(jnp.int32(1), 30 - it)
        return jnp.where(count(key >= cand) >= kf, cand, t)

    t = lax.fori_loop(0, 31, value_bit, t)
    gt = key > t
    eq = key == t
    need = kf - count(gt)

    def index_bit(it, j):
        cand = j | jnp.left_shift(jnp.int32(1), n_idx_bits - 1 - it)
        return jnp.where(count(eq & (idx < cand)) < need, cand, j)

    j = lax.fori_loop(0, n_idx_bits, index_bit, jnp.zeros_like(t))
    return gt | (eq & (idx <= j))


def _indexer_prompt_kernel(iq_ref, ik_ref, iw_ref, o_ref, *, tq, topk):
    seq = ik_ref.shape[0]
    ikb = ik_ref[...].astype(BF16)
    score = jnp.zeros((tq, seq), F32)
    for h in range(IDX_HEADS):
        r = lax.dot_general(iq_ref[:, h * IDX_DIM:(h + 1) * IDX_DIM], ikb, NT_DIMS,
                            preferred_element_type=F32)
        score = score + iw_ref[:, h:h + 1] * jnp.maximum(r, 0.0)
    qpos = pl.program_id(1) * tq + lax.broadcasted_iota(I32, (tq, seq), 0)
    kpos = lax.broadcasted_iota(I32, (tq, seq), 1)
    causal = kpos <= qpos
    key = _order_key(jnp.where(causal, score, -jnp.inf))
    count = lambda m: jnp.sum(m.astype(F32), axis=1, keepdims=True)
    sel = _select_topk(count, key, kpos, topk, int(seq).bit_length())
    o_ref[...] = jnp.where(sel & causal, 0.0, -jnp.inf).astype(o_ref.dtype)


def _indexer_prompt(iq, small, batch, seq, tq, topk):
    nq = seq // tq
    return pl.pallas_call(
        functools.partial(_indexer_prompt_kernel, tq=tq, topk=topk),
        grid=(batch, nq),
        in_specs=[pl.BlockSpec((tq, IDX_HEADS * IDX_DIM), lambda b, i: (b * nq + i, 0)),
                  pl.BlockSpec((seq, LANE), lambda b, i: (b, 2)),
                  pl.BlockSpec((tq, LANE), lambda b, i: (b * nq + i, 1))],
        out_specs=pl.BlockSpec((tq, seq), lambda b, i: (b * nq + i, 0)),
        out_shape=jax.ShapeDtypeStruct((batch * seq, seq), BF16),
        compiler_params=_cparams(("parallel", "parallel")),
        name="indexer_prompt",
    )(iq, small, small)


def _outproj_kernel(a_ref, b_ref, c_ref, wa_ref, wb_ref, wc_ref, y_ref):
    y = jnp.dot(a_ref[...], wa_ref[...], preferred_element_type=F32)
    y = y + jnp.dot(b_ref[...], wb_ref[...], preferred_element_type=F32)
    y = y + jnp.dot(c_ref[...], wc_ref[...], preferred_element_type=F32)
    y_ref[...] = y


def _outproj(a, b, c, wa, wb, wc, tm, tn):
    m = a.shape[0]
    n = wa.shape[1]
    lhs = lambda w: pl.BlockSpec((tm, w), lambda i, j: (i, 0))
    rhs = lambda w: pl.BlockSpec((w, tn), lambda i, j: (0, j))
    return pl.pallas_call(
        _outproj_kernel,
        grid=(m // tm, n // tn),
        in_specs=[lhs(A_WIDTH), lhs(B_WIDTH), lhs(C_WIDTH), rhs(A_WIDTH), rhs(B_WIDTH), rhs(C_WIDTH)],
        out_specs=pl.BlockSpec((tm, tn), lambda i, j: (i, j)),
        out_shape=jax.ShapeDtypeStruct((m, n), F32),
        compiler_params=_cparams(("parallel", "parallel")),
        name="outproj",
    )(a, b, c, wa, wb, wc)


def _postnorm_kernel(x_ref, y_ref, g_ref, o_ref):
    y = y_ref[...]
    ms = jnp.mean(y * y, axis=-1, keepdims=True)
    o_ref[...] = x_ref[...] + y * lax.rsqrt(ms + EPS) * g_ref[...]


def _postnorm(x, y, g, tm):
    m, d = x.shape
    spec = pl.BlockSpec((tm, d), lambda i: (i, 0))
    return pl.pallas_call(
        _postnorm_kernel,
        grid=(m // tm,),
        in_specs=[spec, spec, pl.BlockSpec((1, d), lambda i: (0, 0))],
        out_specs=spec,
        out_shape=jax.ShapeDtypeStruct((m, d), F32),
        compiler_params=_cparams(("parallel",)),
        name="postnorm",
    )(x, y, g.reshape(1, d))


def _suffix_scan(x, axis):
    n = x.shape[axis]
    idx = lax.broadcasted_iota(I32, x.shape, axis)
    k = 1
    while k < n:
        x = x + jnp.where(idx + k < n, pltpu.roll(x, n - k, axis=axis), 0.0)
        k *= 2
    return x


def _logf_bias_kernel(pt_ref, lf_ref, init_ref, o_ref, g_ref, *, n_pages):
    b = pl.program_id(0)

    def gather(j, carry):
        row = pt_ref[b, j]
        for h in range(A_HEADS):
            g_ref[h, pl.ds(j, 1), :] = lf_ref[h, pl.ds(row, 1), :]
        return carry

    lax.fori_loop(0, n_pages, gather, 0)
    for h in range(A_HEADS):
        x = g_ref[h]
        in_page = _suffix_scan(x, 1) - x
        tot = jnp.broadcast_to(jnp.sum(x, axis=1, keepdims=True), x.shape)
        later_pages = _suffix_scan(tot, 0) - tot
        o_ref[h] = init_ref[h:h + 1, :] + later_pages + in_page
    for h in range(A_HEADS, HEAD_ROWS):
        o_ref[h] = jnp.zeros(o_ref.shape[1:], F32)


def _logf_bias(page_table, logf_hp, layer, init):
    bd, n_pages = page_table.shape
    n_pool = logf_hp.shape[2]
    grid_spec = pltpu.PrefetchScalarGridSpec(
        num_scalar_prefetch=1,
        grid=(bd,),
        in_specs=[pl.BlockSpec((None, A_HEADS, n_pool, PAGE_SIZE), lambda b, pt: (layer, 0, 0, 0)),
                  pl.BlockSpec((None, HEAD_ROWS, LANE), lambda b, pt: (b, 0, 0))],
        out_specs=pl.BlockSpec((None, HEAD_ROWS, n_pages, PAGE_SIZE), lambda b, pt: (b, 0, 0, 0)),
        scratch_shapes=[pltpu.VMEM((A_HEADS, n_pages, PAGE_SIZE), F32)],
    )
    return pl.pallas_call(
        functools.partial(_logf_bias_kernel, n_pages=n_pages),
        grid_spec=grid_spec,
        out_shape=jax.ShapeDtypeStruct((bd, HEAD_ROWS, n_pages, PAGE_SIZE), F32),
        compiler_params=_cparams(("arbitrary",)),
        name="logf_bias_sample",
    )(page_table, logf_hp, init)


def _indexer_sample_kernel(pt_ref, *refs, topk, n_pages, pages_per_step):
    ik_refs = refs[:pages_per_step]
    iq_ref, iw_ref, ikn_ref, mb_ref, nb_ref, sc_ref = refs[pages_per_step:]
    j = pl.program_id(1)
    iq = iq_ref[...]
    iw = iw_ref[...]
    rows = []
    for ik_ref in ik_refs:
        r = lax.dot_general(iq, ik_ref[...].astype(BF16), NT_DIMS, preferred_element_type=F32)
        rows.append(jnp.sum(iw * jnp.maximum(r, 0.0), axis=0, keepdims=True))
    first = pl.multiple_of(j * pages_per_step, pages_per_step)
    sc_ref[pl.ds(first, pages_per_step), :] = jnp.concatenate(rows, axis=0)

    @pl.when(j == n_pages // pages_per_step - 1)
    def _():
        rn = jnp.sum(iq.astype(F32) * ikn_ref[...].astype(F32), axis=1, keepdims=True)
        sn = jnp.sum(iw[:, 0:1] * jnp.maximum(rn, 0.0), axis=0, keepdims=True)
        key = _order_key(sc_ref[...])
        key_n = _order_key(sn)
        shape = key.shape
        idx = lax.broadcasted_iota(I32, shape, 0) * PAGE_SIZE + lax.broadcasted_iota(I32, shape, 1)
        idx_n = jnp.full((1, 1), n_pages * PAGE_SIZE, I32)

        def total(m):
            return jnp.sum(jnp.sum(m.astype(F32), axis=1, keepdims=True), axis=0, keepdims=True)

        keys = (key, key_n)
        idxs = (idx, idx_n)
        kf = float(topk)
        cnt_ge = lambda c: total(keys[0] >= c) + total(keys[1] >= c)
        t = jnp.where(cnt_ge(0) >= kf, jnp.int32(0), jnp.int32(INT_MIN))

        def value_bit(it, t):
            cand = t | jnp.left_shift(jnp.int32(1), 30 - it)
            return jnp.where(cnt_ge(cand) >= kf, cand, t)

        t = lax.fori_loop(0, 31, value_bit, t)
        need = kf - (total(keys[0] > t) + total(keys[1] > t))
        n_bits = int(n_pages * PAGE_SIZE + 1).bit_length()

        def index_bit(it, jj):
            cand = jj | jnp.left_shift(jnp.int32(1), n_bits - 1 - it)
            f = total((keys[0] == t) & (idxs[0] < cand)) + total((keys[1] == t) & (idxs[1] < cand))
            return jnp.where(f < need, cand, jj)

        jj = lax.fori_loop(0, n_bits, index_bit, jnp.zeros_like(t))
        sel = (key > t) | ((key == t) & (idx <= jj))
        sel_n = (key_n > t) | ((key_n == t) & (idx_n <= jj))
        mb_ref[...] = jnp.where(sel, 0.0, -jnp.inf)
        nb_ref[...] = jnp.broadcast_to(jnp.where(sel_n, 0.0, -jnp.inf), nb_ref.shape)


def _page_spec(block, layer, pages_per_step, i, last_step=None):
    def index_map(b, j, pt):
        step = j if last_step is None else jnp.minimum(j, last_step)
        return (layer, pt[b, step * pages_per_step + i], 0, 0)
    return pl.BlockSpec((None, None) + block, index_map)


def _indexer_sample(page_table, cache_idxk, layer, iq, iw, ik_new, topk):
    bd, n_pages = page_table.shape
    pps = SAMPLE_PAGES_PER_STEP
    grid_spec = pltpu.PrefetchScalarGridSpec(
        num_scalar_prefetch=1,
        grid=(bd, n_pages // pps),
        in_specs=[_page_spec((PAGE_SIZE, IDX_DIM), layer, pps, i) for i in range(pps)] + [
                  pl.BlockSpec((None, IDX_HEADS, IDX_DIM), lambda b, j, pt: (b, 0, 0)),
                  pl.BlockSpec((None, IDX_HEADS, LANE), lambda b, j, pt: (b, 0, 0)),
                  pl.BlockSpec((None, 1, IDX_DIM), lambda b, j, pt: (b, 0, 0))],
        out_specs=[pl.BlockSpec((None, n_pages, PAGE_SIZE), lambda b, j, pt: (b, 0, 0)),
                   pl.BlockSpec((None, 8, LANE), lambda b, j, pt: (b, 0, 0))],
        scratch_shapes=[pltpu.VMEM((n_pages, PAGE_SIZE), F32)],
    )
    return pl.pallas_call(
        functools.partial(_indexer_sample_kernel, topk=topk, n_pages=n_pages, pages_per_step=pps),
        grid_spec=grid_spec,
        out_shape=[jax.ShapeDtypeStruct((bd, n_pages, PAGE_SIZE), F32),
                   jax.ShapeDtypeStruct((bd, 8, LANE), F32)],
        compiler_params=_cparams(("parallel", "arbitrary")),
        name="indexer_sample",
    )(page_table, *([cache_idxk] * pps), iq, iw, ik_new)


def _attn_sample_kernel(pt_ref, q_ref, *refs, n_steps, pages_per_step, n_heads):
    page_refs = refs[:pages_per_step]
    bias_ref, nk_ref, nv_ref, nbias_ref, g_ref, o_ref, m_ref, l_ref, acc_ref = refs[pages_per_step:]
    j = pl.program_id(1)
    row_stride = 2 * n_heads

    @pl.when(j == 0)
    def _():
        m_ref[...] = jnp.full(m_ref.shape, NEG_BIG, F32)
        l_ref[...] = jnp.zeros(l_ref.shape, F32)
        acc_ref[...] = jnp.zeros(acc_ref.shape, F32)

    def update(n_pages, k_of, v_of, bias_of):
        scores = []
        for i in range(n_pages):
            s = None
            for h in range(n_heads):
                t = lax.dot_general(q_ref[h], k_of(i, h), NT_DIMS, preferred_element_type=F32)
                s = t if s is None else s + t
            scores.append(s * SCALE + bias_of(i))
        m_loc = scores[0]
        for s in scores[1:]:
            m_loc = jnp.maximum(m_loc, s)
        m_old = m_ref[...]
        m_new = jnp.maximum(m_old, jnp.max(m_loc, axis=1, keepdims=True))
        alpha = jnp.exp(m_old - m_new)
        probs = [jnp.exp(s - m_new) for s in scores]
        p_sum = probs[0]
        for p in probs[1:]:
            p_sum = p_sum + p
        l_ref[...] = alpha * l_ref[...] + jnp.sum(p_sum, axis=1, keepdims=True)
        probs = [p.astype(BF16) for p in probs]
        for h in range(n_heads):
            o = None
            for i in range(n_pages):
                t = jnp.dot(probs[i], v_of(i, h), preferred_element_type=F32)
                o = t if o is None else o + t
            acc_ref[h] = alpha * acc_ref[h] + o
        m_ref[...] = m_new

    @pl.when(j < n_steps)
    def _():
        update(pages_per_step,
               lambda i, h: page_refs[i][pl.ds(2 * h, PAGE_SIZE, stride=row_stride), :].astype(BF16),
               lambda i, h: page_refs[i][pl.ds(2 * h + 1, PAGE_SIZE, stride=row_stride), :].astype(BF16),
               lambda i: bias_ref[:, i, :])

    @pl.when(j == n_steps)
    def _():
        update(1, lambda i, h: nk_ref[h], lambda i, h: nv_ref[h], lambda i: nbias_ref[...])
        inv_l = 1.0 / l_ref[...]
        for h in range(n_heads):
            sl = slice(h * HEAD_DIM, (h + 1) * HEAD_DIM)
            o_ref[:, sl] = acc_ref[h, h:h + 1, :] * inv_l[h:h + 1, :] * _silu(g_ref[:, sl])


def _attn_sample(page_table, q_rows, cache_rows, layer, bias, new_k, new_v, new_bias, gate):
    bd, n_pages = page_table.shape
    n_heads = q_rows.shape[1]
    width = n_heads * HEAD_DIM
    bias_rows = bias.shape[1]
    pps = SAMPLE_PAGES_PER_STEP
    n_steps = n_pages // pps
    page_rows = cache_rows.shape[2]
    per_seq = lambda *blk: pl.BlockSpec((None,) + blk, lambda b, j, pt: (b,) + (0,) * len(blk))
    grid_spec = pltpu.PrefetchScalarGridSpec(
        num_scalar_prefetch=1,
        grid=(bd, n_steps + 1),
        in_specs=[per_seq(n_heads, HEAD_ROWS, HEAD_DIM)]
                 + [_page_spec((page_rows, HEAD_DIM), layer, pps, i, n_steps - 1) for i in range(pps)]
                 + [pl.BlockSpec((None, bias_rows, pps, PAGE_SIZE),
                                 lambda b, j, pt: (b, 0, jnp.minimum(j, n_steps - 1), 0)),
                    per_seq(n_heads, PAGE_SIZE, HEAD_DIM),
                    per_seq(n_heads, PAGE_SIZE, HEAD_DIM),
                    per_seq(1, PAGE_SIZE),
                    per_seq(1, width)],
        out_specs=per_seq(1, width),
        scratch_shapes=[pltpu.VMEM((HEAD_ROWS, LANE), F32),
                        pltpu.VMEM((HEAD_ROWS, LANE), F32),
                        pltpu.VMEM((n_heads, HEAD_ROWS, HEAD_DIM), F32)],
    )
    return pl.pallas_call(
        functools.partial(_attn_sample_kernel, n_steps=n_steps, pages_per_step=pps, n_heads=n_heads),
        grid_spec=grid_spec,
        out_shape=jax.ShapeDtypeStruct((bd, 1, width), F32),
        compiler_params=_cparams(("parallel", "arbitrary")),
        name="attn_sample",
    )(page_table, q_rows, *([cache_rows] * pps), bias, new_k, new_v, new_bias, gate)


def _pool_sample_kernel(ext_ref, g_ref, w_ref, sc_ref, o_ref, *, past_len):
    hist = POOL_HIST + 1
    for g, w in enumerate(POOL_WINDOWS):
        sl = slice(g * POOL_GW, (g + 1) * POOL_GW)
        sw = jnp.sum(ext_ref[hist - w:hist, :, sl], axis=0)
        u = ext_ref[hist - 1, :, sl]
        d = sw / float(min(w, past_len + 1)) - u
        y = jnp.dot(d.astype(BF16), w_ref[g], preferred_element_type=F32) * sc_ref[:, sl]
        o_ref[:, sl] = y * _silu(g_ref[:, sl])


def _pool_sample(ext_t, gate, w_pool, pool_scale, past_len):
    rows = ext_t.shape[1]
    return pl.pallas_call(
        functools.partial(_pool_sample_kernel, past_len=past_len),
        out_shape=jax.ShapeDtypeStruct((rows, B_WIDTH), F32),
        compiler_params=pltpu.CompilerParams(vmem_limit_bytes=VMEM_LIMIT),
        name="pool_sample",
    )(ext_t, gate, w_pool, pool_scale)


def _rope_tables(pos):
    half = HEAD_DIM // 2
    inv = jnp.power(ROPE_THETA, -jnp.arange(half, dtype=F32) / half)
    ang = pos.astype(F32)[:, None] * inv[None, :]
    cos, sin = jnp.cos(ang), jnp.sin(ang)
    return jnp.concatenate([cos, cos], axis=1), jnp.concatenate([-sin, sin], axis=1)


def _prep_weights(w_in, b_f, w_pool, w_out):
    col = lambda a, n: w_in[:, a:a + n].astype(BF16)
    d = w_in.shape[0]

    def kv_cols(ok, ov, n_heads):
        k = w_in[:, ok:ok + n_heads * HEAD_DIM].reshape(d, n_heads, 1, HEAD_DIM)
        v = w_in[:, ov:ov + n_heads * HEAD_DIM].reshape(d, n_heads, 1, HEAD_DIM)
        return jnp.concatenate([k, v], axis=2).reshape(d, 2 * n_heads * HEAD_DIM).astype(BF16)

    small = jnp.concatenate(
        [w_in[:, O_AF:O_AF + A_HEADS], jnp.zeros((d, LANE - A_HEADS), F32),
         w_in[:, O_IW:O_IW + IDX_HEADS], jnp.zeros((d, LANE - IDX_HEADS), F32),
         w_in[:, O_IK:O_IK + IDX_DIM]], axis=1).astype(BF16)
    return dict(
        aq=col(O_AQ, A_WIDTH), akv=kv_cols(O_AK, O_AV, A_HEADS), ag=col(O_AG, A_WIDTH),
        b=col(O_BU, 2 * B_WIDTH),
        cq=col(O_CQ, C_WIDTH), ckv=kv_cols(O_CK, O_CV, C_HEADS), cg=col(O_CG, C_WIDTH),
        iq=col(O_IQ, IDX_HEADS * IDX_DIM), small=small,
        bf=jnp.pad(b_f, (0, LANE - A_HEADS)).reshape(1, LANE),
        pool=w_pool.astype(BF16),
        oa=w_out[0:A_WIDTH].astype(BF16),
        ob=w_out[A_WIDTH:A_WIDTH + B_WIDTH].astype(BF16),
        oc=w_out[A_WIDTH + B_WIDTH:].astype(BF16),
    )


def _project_all(h, w, rope, tm, tn):
    p = {}
    (p["aq"],) = _proj(h, w["aq"], [BF16], tm, tn)
    p["akv"], p["akv_b"] = _proj(h, w["akv"], [F32, BF16], tm, KV_TN, mode="kv")
    (p["ag"],) = _proj(h, w["ag"], [F32], tm, tn)
    (p["b"],) = _proj(h, w["b"], [F32], tm, tn)
    (p["cq"],) = _proj(h, w["cq"], [BF16], tm, tn, mode="rope", rope=rope)
    p["ckv"], p["ckv_b"] = _proj(h, w["ckv"], [F32, BF16], tm, KV_TN, mode="kv_rope", rope=rope)
    (p["cg"],) = _proj(h, w["cg"], [F32], tm, tn)
    (p["iq"],) = _proj(h, w["iq"], [BF16], tm, tn, mode="rope", rope=rope)
    (p["small"],) = _proj(h, w["small"], [F32], tm, SMALL_W, mode="small", rope=rope, bias=w["bf"])
    return p


def _q_rows(q, n_heads):
    rows = q.shape[0]
    eye = jnp.eye(n_heads, HEAD_ROWS, dtype=q.dtype)
    return q.reshape(rows, n_heads, 1, HEAD_DIM) * eye[None, :, :, None]


def _new_token_kv(kv_rows, n_heads):
    bd = kv_rows.shape[0]
    kv = kv_rows.reshape(bd, n_heads, 2, 1, HEAD_DIM).astype(BF16)
    pad = ((0, 0), (0, 0), (0, PAGE_SIZE - 1), (0, 0))
    return jnp.pad(kv[:, :, 0], pad), jnp.pad(kv[:, :, 1], pad)


def _cache_rows(cache):
    d, n, p, two, h, hd = cache.shape
    return jnp.transpose(cache, (0, 1, 2, 4, 3, 5)).reshape(d, n, p * h * two, hd)


def _cache_layout(kv_rows, lead, n_heads):
    t = kv_rows.reshape(lead + (n_heads, 2, HEAD_DIM))
    return jnp.swapaxes(t, -3, -2)


def kernel(x_prompt, x_sample, cache_a_kv, cache_a_logf, cache_c_kv, cache_c_idxk, state_pool,
           page_table, g_pre, w_in, b_f, w_pool, pool_scale, w_out, g_post):
    batch, seq, d_model = x_prompt.shape
    bd = x_sample.shape[0]
    depth = g_pre.shape[0]
    n_pool = cache_a_kv.shape[1]
    n_pages = page_table.shape[1]
    past_len = n_pages * PAGE_SIZE
    m_p = batch * seq
    tm_p = min(1024, m_p)
    tn = 512
    tq = min(256, seq)
    topk_p = min(TOPK_MAX, seq // 4)
    topk_s = min(TOPK_MAX, (past_len + 1) // 4)

    rope_p = _rope_tables(jnp.arange(seq))
    rope_s = _rope_tables(jnp.full((SAMPLE_ROWS,), past_len))
    lane0 = lax.broadcasted_iota(I32, (bd, 1, PAGE_SIZE), 2) == 0

    assert x_sample.shape[1] == 1 and n_pages % SAMPLE_PAGES_PER_STEP == 0 and n_pool == cache_c_kv.shape[1]
    cache_a = _cache_rows(cache_a_kv)
    cache_c = _cache_rows(cache_c_kv)
    logf_hp = jnp.transpose(cache_a_logf, (0, 3, 1, 2))

    xp = x_prompt.reshape(m_p, d_model)
    xs = jnp.pad(x_sample.reshape(bd, d_model), ((0, SAMPLE_ROWS - bd), (0, 0)))

    outs = {k: [] for k in ("p_akv", "p_alogf", "p_ckv", "p_cik", "p_pool",
                            "s_akv", "s_alogf", "s_ckv", "s_cik", "s_pool")}
    for l in range(depth):
        w = _prep_weights(w_in[l], b_f[l], w_pool[l], w_out[l])
        scale_row = pool_scale[l].reshape(1, B_WIDTH)

        h = _rmsnorm(xp, g_pre[l], 256)
        p = _project_all(h, w, rope_p, tm_p, tn)
        c = _cumsum_logf(p["small"], batch, seq)
        c_t = jnp.pad(jnp.swapaxes(c.reshape(batch, seq, LANE)[:, :, :A_HEADS], 1, 2),
                      ((0, 0), (0, HEAD_ROWS - A_HEADS), (0, 0)))
        mix_a = _attn_prompt(p["aq"], p["akv_b"], p["ag"], (c, c_t), "fox", batch, seq, tq)
        mix_b = _pool_prompt(p["b"], w["pool"], scale_row, batch, seq)
        mask_bias = _indexer_prompt(p["iq"], p["small"], batch, seq, tq, topk_p)
        mix_c = _attn_prompt(p["cq"], p["ckv_b"], p["cg"], mask_bias, "dsa", batch, seq, tq)
        y = _outproj(mix_a, mix_b, mix_c, w["oa"], w["ob"], w["oc"], tm_p, tn)
        xp = _postnorm(xp, y, g_post[l], 256)

        outs["p_akv"].append(_cache_layout(p["akv"], (batch, seq), A_HEADS))
        outs["p_alogf"].append(p["small"][:, :A_HEADS].reshape(batch, seq, A_HEADS))
        outs["p_ckv"].append(_cache_layout(p["ckv"], (batch, seq), C_HEADS))
        outs["p_cik"].append(p["small"][:, 2 * LANE:].reshape(batch, seq, IDX_DIM))
        outs["p_pool"].append(p["b"].reshape(batch, seq, 2 * B_WIDTH)[:, seq - POOL_HIST:, :B_WIDTH])

        hs = _rmsnorm(xs, g_pre[l], SAMPLE_ROWS)
        s = _project_all(hs, w, rope_s, SAMPLE_ROWS, tn)
        logf_s = s["small"][:bd, :A_HEADS]
        init = jnp.broadcast_to(jnp.pad(logf_s, ((0, 0), (0, HEAD_ROWS - A_HEADS)))[:, :, None],
                                (bd, HEAD_ROWS, LANE))
        fox_bias = _logf_bias(page_table, logf_hp, l, init)
        new_bias_a = jnp.where(lane0, 0.0, -jnp.inf).astype(F32)
        sa = _attn_sample(page_table, _q_rows(s["aq"][:bd], A_HEADS), cache_a, l, fox_bias,
                          *_new_token_kv(s["akv"][:bd], A_HEADS), new_bias_a, s["ag"][:bd, None, :])

        iq_s = s["iq"][:bd].reshape(bd, IDX_HEADS, IDX_DIM)
        iw_s = jnp.broadcast_to(s["small"][:bd, LANE:LANE + IDX_HEADS, None], (bd, IDX_HEADS, LANE))
        ikn_s = s["small"][:bd, None, 2 * LANE:].astype(BF16)
        dsa_bias, nb = _indexer_sample(page_table, cache_c_idxk, l, iq_s, iw_s, ikn_s, topk_s)
        new_bias_c = jnp.where(lane0, nb[:, 0:1, :], -jnp.inf)
        sc = _attn_sample(page_table, _q_rows(s["cq"][:bd], C_HEADS), cache_c, l,
                          dsa_bias.reshape(bd, 1, n_pages, PAGE_SIZE),
                          *_new_token_kv(s["ckv"][:bd], C_HEADS), new_bias_c, s["cg"][:bd, None, :])

        bu_s = s["b"][:bd, :B_WIDTH]
        ext = jnp.concatenate([state_pool[l], bu_s[:, None, :]], axis=1)
        ext_t = jnp.pad(jnp.swapaxes(ext, 0, 1), ((0, 0), (0, SAMPLE_ROWS - bd), (0, 0)))
        sb = _pool_sample(ext_t, s["b"][:, B_WIDTH:], w["pool"], scale_row, past_len)

        pad_rows = lambda t: jnp.pad(t, ((0, SAMPLE_ROWS - bd), (0, 0))).astype(BF16)
        ys = _outproj(pad_rows(sa[:, 0, :]), sb.astype(BF16), pad_rows(sc[:, 0, :]),
                      w["oa"], w["ob"], w["oc"], SAMPLE_ROWS, tn)
        xs = _postnorm(xs, ys, g_post[l], SAMPLE_ROWS)

        outs["s_akv"].append(_cache_layout(s["akv"][:bd], (bd, 1), A_HEADS))
        outs["s_alogf"].append(logf_s.reshape(bd, 1, A_HEADS))
        outs["s_ckv"].append(_cache_layout(s["ckv"][:bd], (bd, 1), C_HEADS))
        outs["s_cik"].append(s["small"][:bd, 2 * LANE:].reshape(bd, 1, IDX_DIM))
        outs["s_pool"].append(ext[:, 1:, :])

    st = lambda k: jnp.stack(outs[k])
    return (xp.reshape(batch, seq, d_model), xs[:bd].reshape(bd, 1, d_model),
            st("p_akv"), st("p_alogf"), st("p_ckv"), st("p_cik"), st("p_pool"),
            st("s_akv"), st("s_alogf"), st("s_ckv"), st("s_cik"), st("s_pool"))
```
